```python
import math
import jax, jax.numpy as jnp
from jax import lax
import numpy as np

D_MODEL = 2048
BATCH = 2
SEQ = 16384
DEPTH = 4
DEC_BATCH = 8
DEC_SEQ = 4096
PAST_LEN = 128

HEAD_DIM = 128
N_HEADS = 8
N_KV = 2
GQA_G = N_HEADS // N_KV
ATTN_W = N_HEADS * HEAD_DIM
KV_W = N_KV * HEAD_DIM
C_CONV = D_MODEL - ATTN_W
MIX_W = ATTN_W + C_CONV
IN_W = ATTN_W + 2 * KV_W + 2 * C_CONV
CONV_K = 31
WINDOW = 128
BLK = 128
ROT_DIM = HEAD_DIM // 4
ROPE_THETA = 500000.0
D_FF = ((int(math.ceil(8 * D_MODEL / 3)) + 255) // 256) * 256
PLE_DIM = 256
EPS = 1e-6

kernel_name = "hymba_conformer_swa_encoder"


def rmsnorm(x, g):
    xf = x.astype(jnp.float32)
    y = xf * lax.rsqrt(jnp.mean(xf * xf, axis=-1, keepdims=True) + EPS)
    return (y * g.astype(jnp.float32)).astype(x.dtype)


def layernorm(x, g, b):
    xf = x.astype(jnp.float32)
    mu = jnp.mean(xf, axis=-1, keepdims=True)
    var = jnp.mean(jnp.square(xf - mu), axis=-1, keepdims=True)
    y = (xf - mu) * lax.rsqrt(var + EPS)
    return (y * g.astype(jnp.float32) + b.astype(jnp.float32)).astype(x.dtype)


def partial_rope(x, pos):
    half = ROT_DIM // 2
    inv_freq = jnp.exp(-math.log(ROPE_THETA) * jnp.arange(0, ROT_DIM, 2, dtype=jnp.float32) / ROT_DIM)
    ang = pos[:, None] * inv_freq[None, :]
    cos = jnp.cos(ang)[:, None, :]
    sin = jnp.sin(ang)[:, None, :]
    xr = x[..., :ROT_DIM].astype(jnp.float32)
    x1, x2 = xr[..., :half], xr[..., half:]
    rot = jnp.concatenate([x1 * cos - x2 * sin, x2 * cos + x1 * sin], axis=-1).astype(x.dtype)
    return jnp.concatenate([rot, x[..., ROT_DIM:]], axis=-1)


def banded_sink_attention(q, k, v, sink):
    B, S = q.shape[0], q.shape[1]
    nb = S // BLK
    qb = (q * (HEAD_DIM ** -0.5)).reshape(B, nb, BLK, N_KV, GQA_G, HEAD_DIM)
    pad = ((0, 0), (WINDOW, WINDOW), (0, 0), (0, 0))
    kp = jnp.pad(k, pad).reshape(B, nb + 2, BLK, N_KV, HEAD_DIM)
    vp = jnp.pad(v, pad).reshape(B, nb + 2, BLK, N_KV, HEAD_DIM)
    kb = jnp.concatenate([kp[:, :-2], kp[:, 1:-1], kp[:, 2:]], axis=2)
    vb = jnp.concatenate([vp[:, :-2], vp[:, 1:-1], vp[:, 2:]], axis=2)
    s = jnp.einsum('bnqhgd,bnkhd->bnhgqk', qb, kb, preferred_element_type=jnp.float32)
    n_idx = jnp.arange(nb)[:, None, None]
    t_idx = jnp.arange(BLK)[None, :, None]
    s_idx = jnp.arange(3 * BLK)[None, None, :]
    kpos = (n_idx - 1) * BLK + s_idx
    rel = s_idx - BLK - t_idx
    valid = (jnp.abs(rel) <= WINDOW) & (kpos >= 0) & (kpos < S)
    s = jnp.where(valid[None, :, None, None], s, -jnp.inf)
    sk = sink.astype(jnp.float32).reshape(1, 1, N_KV, GQA_G, 1)
    m = jnp.maximum(jnp.max(s, axis=-1), sk)
    e = jnp.exp(s - m[..., None])
    denom = jnp.sum(e, axis=-1) + jnp.exp(sk - m)
    probs = (e / denom[..., None]).astype(v.dtype)
    out = jnp.einsum('bnhgqk,bnkhd->bnqhgd', probs, vb, preferred_element_type=jnp.float32)
    return out.astype(q.dtype).reshape(B, S, ATTN_W)


def conformer_conv(u, conv_w, conv_b, ln_g, ln_b):
    a, b = u[..., :C_CONV], u[..., C_CONV:]
    g = a * jax.nn.sigmoid(b)
    y = lax.conv_general_dilated(
        g, conv_w[:, None, :].astype(g.dtype), window_strides=(1,),
        padding=[((CONV_K - 1) // 2, (CONV_K - 1) // 2)],
        dimension_numbers=('NWC', 'WIO', 'NWC'), feature_group_count=C_CONV)
    y = y + conv_b
    return jax.nn.silu(layernorm(y, ln_g, ln_b))


def encoder_layer(h, p_l, g_mix, w_in, sink, conv_w, conv_b, conv_ln_g, conv_ln_b,
                  g_attn_out, g_conv_out, w_out, g_ffn, w_gate, w_up, w_down,
                  g_ple, w_ple_gate, w_ple):
    B, S, _ = h.shape
    a = rmsnorm(h, g_mix)
    z = a @ w_in
    q = z[..., :ATTN_W].reshape(B, S, N_HEADS, HEAD_DIM)
    k = z[..., ATTN_W:ATTN_W + KV_W].reshape(B, S, N_KV, HEAD_DIM)
    v = z[..., ATTN_W + KV_W:ATTN_W + 2 * KV_W].reshape(B, S, N_KV, HEAD_DIM)
    u = z[..., ATTN_W + 2 * KV_W:]
    pos = jnp.arange(S, dtype=jnp.float32)
    attn = banded_sink_attention(partial_rope(q, pos), partial_rope(k, pos), v, sink)
    conv = conformer_conv(u, conv_w, conv_b, conv_ln_g, conv_ln_b)
    mixed = jnp.concatenate([rmsnorm(attn, g_attn_out), rmsnorm(conv, g_conv_out)], axis=-1)
    h = h + mixed @ w_out
    f = rmsnorm(h, g_ffn)
    h = h + (jax.nn.silu(f @ w_gate) * (f @ w_up)) @ w_down
    gate = jax.nn.sigmoid(rmsnorm(h, g_ple) @ w_ple_gate)
    return h + (p_l @ w_ple) * gate


def setup_inputs(seed: int = 0) -> dict:
    key = jax.random.key(seed)
    ks = jax.random.split(key, 24)
    f32 = jnp.float32

    def nrm(k, shape, scale):
        return jax.random.normal(k, shape, f32) * scale

    def gain(k, shape):
        return 1.0 + 0.05 * jax.random.normal(k, shape, f32)

    return {
        "x_prompt": nrm(ks[0], (BATCH, SEQ, D_MODEL), 1.0),
        "x_sample": nrm(ks[1], (DEC_BATCH, DEC_SEQ, D_MODEL), 1.0),
        "p_prompt": nrm(ks[2], (DEPTH, BATCH, SEQ, PLE_DIM), 1.0),
        "p_sample": nrm(ks[3], (DEPTH, DEC_BATCH, DEC_SEQ, PLE_DIM), 1.0),
        "g_mix": gain(ks[4], (DEPTH, D_MODEL)),
        "w_in": nrm(ks[5], (DEPTH, D_MODEL, IN_W), D_MODEL ** -0.5),
        "sink": nrm(ks[6], (DEPTH, N_HEADS), 0.5),
        "conv_w": nrm(ks[7], (DEPTH, CONV_K, C_CONV), CONV_K ** -0.5),
        "conv_b": nrm(ks[8], (DEPTH, C_CONV), 0.02),
        "conv_ln_g": gain(ks[9], (DEPTH, C_CONV)),
        "conv_ln_b": nrm(ks[10], (DEPTH, C_CONV), 0.02),
        "g_attn_out": gain(ks[11], (DEPTH, ATTN_W)),
        "g_conv_out": gain(ks[12], (DEPTH, C_CONV)),
        "w_out": nrm(ks[13], (DEPTH, MIX_W, D_MODEL), MIX_W ** -0.5),
        "g_ffn": gain(ks[14], (DEPTH, D_MODEL)),
        "w_gate": nrm(ks[15], (DEPTH, D_MODEL, D_FF), D_MODEL ** -0.5),
        "w_up": nrm(ks[16], (DEPTH, D_MODEL, D_FF), D_MODEL ** -0.5),
        "w_down": nrm(ks[17], (DEPTH, D_FF, D_MODEL), D_FF ** -0.5),
        "g_ple": gain(ks[18], (DEPTH, D_MODEL)),
        "w_ple_gate": nrm(ks[19], (DEPTH, D_MODEL, D_MODEL), D_MODEL ** -0.5),
        "w_ple": nrm(ks[20], (DEPTH, PLE_DIM, D_MODEL), PLE_DIM ** -0.5),
        "g_final": gain(ks[21], (D_MODEL,)),
    }


def reference(x_prompt, x_sample, p_prompt, p_sample, g_mix, w_in, sink, conv_w, conv_b,
              conv_ln_g, conv_ln_b, g_attn_out, g_conv_out, w_out, g_ffn, w_gate, w_up,
              w_down, g_ple, w_ple_gate, w_ple, g_final):
    def trunk(x, p):
        h = x
        for i in range(DEPTH):
            h = encoder_layer(h, p[i], g_mix[i], w_in[i], sink[i], conv_w[i], conv_b[i],
                              conv_ln_g[i], conv_ln_b[i], g_attn_out[i], g_conv_out[i],
                              w_out[i], g_ffn[i], w_gate[i], w_up[i], w_down[i],
                              g_ple[i], w_ple_gate[i], w_ple[i])
        return rmsnorm(h, g_final)

    y_prompt = trunk(x_prompt, p_prompt)
    y_sample = trunk(x_sample, p_sample)
    return (y_prompt, y_sample)
```

```python
import functools
import math

import numpy as np
import jax
import jax.numpy as jnp
from jax import lax
from jax.experimental import pallas as pl
from jax.experimental.pallas import tpu as pltpu

D_MODEL = 2048
HEAD_DIM = 128
N_HEADS = 8
N_KV = 2
GQA_G = N_HEADS // N_KV
ATTN_W = N_HEADS * HEAD_DIM
KV_W = N_KV * HEAD_DIM
C_CONV = D_MODEL - ATTN_W
IN_W = ATTN_W + 2 * KV_W + 2 * C_CONV
CONV_K = 31
CONV_PAD = (CONV_K - 1) // 2
WINDOW = 128
BLK = 128
ROT_DIM = HEAD_DIM // 4
ROPE_THETA = 500000.0
D_FF = 5632
PLE_DIM = 256
EPS = 1e-6

V7X_VMEM_BYTES = 64 * 1024 * 1024
LANES = 128
CONV_HALO_ROWS = 16

F32 = jnp.float32
BF16 = jnp.bfloat16


def _vmem_limit(estimate_bytes):
    return int(min(max(estimate_bytes * 5 // 4, 32 * 1024 * 1024), V7X_VMEM_BYTES - 6 * 1024 * 1024))


def _resident(shape):
    nd = len(shape)
    return pl.BlockSpec(shape, lambda *_: (0,) * nd, pipeline_mode=pl.Buffered(1))


def _rms_scale(x, g):
    ms = jnp.mean(x * x, axis=-1, keepdims=True)
    return x * lax.rsqrt(ms + EPS) * g


def _sigmoid(x):
    return 1.0 / (1.0 + jnp.exp(-x))


def _dot(a, b):
    return jnp.dot(a, b, preferred_element_type=F32)


def _in_proj_kernel(x_ref, g_ref, w_ref, c_ref, sa_ref, sb_ref, q_ref, k_ref, v_ref, u_ref):
    a = _rms_scale(x_ref[...], g_ref[...]).astype(BF16)
    c, sa, sb = c_ref[...], sa_ref[...], sb_ref[...]

    def rope(xh):
        return (xh * c + pltpu.roll(xh, HEAD_DIM - ROT_DIM // 2, 1) * sa
                + pltpu.roll(xh, ROT_DIM // 2, 1) * sb)

    zq = _dot(a, w_ref[:, 0:ATTN_W])
    scale = HEAD_DIM ** -0.5
    for h in range(N_HEADS):
        sl = slice(h * HEAD_DIM, (h + 1) * HEAD_DIM)
        q_ref[:, sl] = (rope(zq[:, sl]) * scale).astype(BF16)
    zkv = _dot(a, w_ref[:, ATTN_W:ATTN_W + 2 * KV_W])
    for h in range(N_KV):
        sl = slice(h * HEAD_DIM, (h + 1) * HEAD_DIM)
        k_ref[:, sl] = rope(zkv[:, sl]).astype(BF16)
    v_ref[...] = zkv[:, KV_W:2 * KV_W].astype(BF16)
    u0 = ATTN_W + 2 * KV_W
    za = _dot(a, w_ref[:, u0:u0 + C_CONV])
    zb = _dot(a, w_ref[:, u0 + C_CONV:u0 + 2 * C_CONV])
    u_ref[...] = za * _sigmoid(zb)


def _in_proj(x, g_mix, w_in, rope_c, rope_sa, rope_sb, seq):
    m = x.shape[0]
    tm = min(512, seq)
    nseq = seq // tm
    row = lambda i: (i, 0)
    tab = pl.BlockSpec((tm, HEAD_DIM), lambda i: (i % nseq, 0))
    est = (2 * tm * D_MODEL * 4 + D_MODEL * IN_W * 2 + 2 * tm * (ATTN_W * 2 + 2 * KV_W * 2 + C_CONV * 4)
           + tm * D_MODEL * 2 + 3 * tm * C_CONV * 4)
    return pl.pallas_call(
        _in_proj_kernel,
        grid=(m // tm,),
        in_specs=[pl.BlockSpec((tm, D_MODEL), row), _resident((1, D_MODEL)), _resident((D_MODEL, IN_W)),
                  tab, tab, tab],
        out_specs=[pl.BlockSpec((tm, ATTN_W), row), pl.BlockSpec((tm, KV_W), row),
                   pl.BlockSpec((tm, KV_W), row), pl.BlockSpec((tm, C_CONV), row)],
        out_shape=[jax.ShapeDtypeStruct((m, ATTN_W), BF16), jax.ShapeDtypeStruct((m, KV_W), BF16),
                   jax.ShapeDtypeStruct((m, KV_W), BF16), jax.ShapeDtypeStruct((m, C_CONV), F32)],
        compiler_params=pltpu.CompilerParams(dimension_semantics=("parallel",),
                                             vmem_limit_bytes=_vmem_limit(est)),
        name="in_proj",
    )(x, g_mix, w_in, rope_c, rope_sa, rope_sb)


def _mix_kernel(sink_ref, h_ref, q_ref, kp_ref, kc_ref, kn_ref, vp_ref, vc_ref, vn_ref,
                up_ref, uc_ref, un_ref, cw_ref, cb_ref, lng_ref, lnb_ref, gao_ref, gco_ref, wout_ref,
                o_ref, kext, vext, uext, attn_s, conv_s, mixed_s, *, tq, nblk, conv_rows):
    i = pl.program_id(1)
    first = i == 0
    last = i == nblk - 1

    kext[0:WINDOW] = kp_ref[...]
    kext[WINDOW:WINDOW + tq] = kc_ref[...]
    kext[WINDOW + tq:] = kn_ref[...]
    vext[0:WINDOW] = vp_ref[...]
    vext[WINDOW:WINDOW + tq] = vc_ref[...]
    vext[WINDOW + tq:] = vn_ref[...]
    hr = CONV_HALO_ROWS
    ncg = C_CONV // LANES
    for cg in range(ncg):
        csl = slice(cg * LANES, (cg + 1) * LANES)
        uext[cg, 0:hr] = jnp.where(first, 0.0, up_ref[:, csl])
        uext[cg, hr:hr + tq] = uc_ref[:, csl]
        uext[cg, hr + tq:] = jnp.where(last, 0.0, un_ref[:, csl])

    rows = GQA_G * BLK
    r_idx = lax.broadcasted_iota(jnp.int32, (rows, 3 * BLK), 0)
    s_idx = lax.broadcasted_iota(jnp.int32, (rows, 3 * BLK), 1)
    rel = s_idx - BLK - (r_idx & (BLK - 1))
    band = jnp.abs(rel) <= WINDOW
    lo = jnp.where(first, BLK, 0)
    hi = jnp.where(last, 2 * BLK, 3 * BLK)
    nsub = tq // BLK
    rcol = lax.broadcasted_iota(jnp.int32, (rows, 1), 0)
    for hk in range(N_KV):
        ksl = slice(hk * HEAD_DIM, (hk + 1) * HEAD_DIM)
        sk = jnp.full((rows, 1), sink_ref[hk * GQA_G + GQA_G - 1], F32)
        for g in range(GQA_G - 2, -1, -1):
            sk = jnp.where(rcol < (g + 1) * BLK, sink_ref[hk * GQA_G + g], sk)
        for j in range(nsub):
            valid = band
            if j == 0:
                valid = valid & (s_idx >= lo)
            if j == nsub - 1:
                valid = valid & (s_idx < hi)
            qs = jnp.concatenate(
                [q_ref[j * BLK:(j + 1) * BLK, (hk * GQA_G + g) * HEAD_DIM:(hk * GQA_G + g + 1) * HEAD_DIM]
                 for g in range(GQA_G)], axis=0)
            kb = kext[j * BLK:(j + 3) * BLK, ksl]
            vb = vext[j * BLK:(j + 3) * BLK, ksl]
            s = lax.dot_general(qs, kb, (((1,), (1,)), ((), ())), preferred_element_type=F32)
            s = jnp.where(valid, s, -jnp.inf)
            m = jnp.maximum(jnp.max(s, axis=-1, keepdims=True), sk)
            e = jnp.exp(s - m)
            denom = jnp.sum(e, axis=-1, keepdims=True) + jnp.exp(sk - m)
            probs = (e * (1.0 / denom)).astype(BF16)
            out = _dot(probs, vb)
            for g in range(GQA_G):
                h = hk * GQA_G + g
                attn_s[j * BLK:(j + 1) * BLK, h * HEAD_DIM:(h + 1) * HEAD_DIM] = out[g * BLK:(g + 1) * BLK]

    def conv_group_body(cg, carry):
        w = cw_ref[cg]
        bias = cb_ref[cg]
        for r0 in range(0, tq, conv_rows):
            acc = jnp.broadcast_to(bias, (conv_rows, LANES))
            for k in range(CONV_K):
                off = r0 + hr - CONV_PAD + k
                acc = acc + uext[cg, off:off + conv_rows, :] * w[k:k + 1, :]
            conv_s[cg, r0:r0 + conv_rows, :] = acc
        return carry

    lax.fori_loop(0, ncg, conv_group_body, 0)

    y = jnp.concatenate([conv_s[cg] for cg in range(ncg)], axis=-1)
    mu = jnp.mean(y, axis=-1, keepdims=True)
    yc = y - mu
    var = jnp.mean(yc * yc, axis=-1, keepdims=True)
    yn = yc * lax.rsqrt(var + EPS) * lng_ref[...] + lnb_ref[...]
    cv = yn * _sigmoid(yn)
    mixed_s[:, ATTN_W:] = _rms_scale(cv, gco_ref[...]).astype(BF16)
    mixed_s[:, 0:ATTN_W] = _rms_scale(attn_s[...], gao_ref[...]).astype(BF16)

    o_ref[...] = h_ref[...] + _dot(mixed_s[...], wout_ref[...])


def _mix(h, q, k, v, u, sink, conv_w, conv_b, ln_g, ln_b, g_attn_out, g_conv_out, w_out):
    b, seq, _ = h.shape
    tq = min(512, seq)
    nblk = seq // tq
    kb = tq // BLK
    ub = tq // CONV_HALO_ROWS
    cur = lambda bi, i: (bi, i, 0)
    kprev = lambda bi, i: (bi, jnp.maximum(i * kb - 1, 0), 0)
    knext = lambda bi, i: (bi, jnp.minimum((i + 1) * kb, seq // BLK - 1), 0)
    uprev = lambda bi, i: (bi, jnp.maximum(i * ub - 1, 0), 0)
    unext = lambda bi, i: (bi, jnp.minimum((i + 1) * ub, seq // CONV_HALO_ROWS - 1), 0)
    est = (4 * tq * D_MODEL * 4 + D_MODEL * D_MODEL * 2 + 2 * tq * ATTN_W * 2 + 8 * (tq + 2 * BLK) * KV_W * 2
           + 3 * (tq + 32) * C_CONV * 4 + 2 * tq * ATTN_W * 4 + tq * D_MODEL * 2 + 6 * tq * C_CONV * 4)
    kernel = functools.partial(_mix_kernel, tq=tq, nblk=nblk, conv_rows=min(64, tq))
    cw = conv_w.reshape(CONV_K, C_CONV // LANES, LANES).transpose(1, 0, 2)
    cb = conv_b.reshape(C_CONV // LANES, 1, LANES)
    return pl.pallas_call(
        kernel,
        grid=(b, nblk),
        in_specs=[pl.BlockSpec(memory_space=pltpu.SMEM),
                  pl.BlockSpec((None, tq, D_MODEL), cur),
                  pl.BlockSpec((None, tq, ATTN_W), cur),
                  pl.BlockSpec((None, BLK, KV_W), kprev), pl.BlockSpec((None, tq, KV_W), cur),
                  pl.BlockSpec((None, BLK, KV_W), knext),
                  pl.BlockSpec((None, BLK, KV_W), kprev), pl.BlockSpec((None, tq, KV_W), cur),
                  pl.BlockSpec((None, BLK, KV_W), knext),
                  pl.BlockSpec((None, CONV_HALO_ROWS, C_CONV), uprev), pl.BlockSpec((None, tq, C_CONV), cur),
                  pl.BlockSpec((None, CONV_HALO_ROWS, C_CONV), unext),
                  _resident((C_CONV // LANES, CONV_K, LANES)), _resident((C_CONV // LANES, 1, LANES)),
                  _resident((1, C_CONV)),
                  _resident((1, C_CONV)), _resident((1, ATTN_W)), _resident((1, C_CONV)),
                  _resident((D_MODEL, D_MODEL))],
        out_specs=pl.BlockSpec((None, tq, D_MODEL), cur),
        out_shape=jax.ShapeDtypeStruct((b, seq, D_MODEL), F32),
        scratch_shapes=[pltpu.VMEM((tq + 2 * BLK, KV_W), BF16), pltpu.VMEM((tq + 2 * BLK, KV_W), BF16),
                        pltpu.VMEM((C_CONV // LANES, tq + 2 * CONV_HALO_ROWS, LANES), F32),
                        pltpu.VMEM((tq, ATTN_W), F32), pltpu.VMEM((C_CONV // LANES, tq, LANES), F32),
                        pltpu.VMEM((tq, D_MODEL), BF16)],
        compiler_params=pltpu.CompilerParams(dimension_semantics=("parallel", "arbitrary"),
                                             vmem_limit_bytes=_vmem_limit(est)),
        name="mix",
    )(sink, h, q, k, k, k, v, v, v, u, u, u, cw, cb, ln_g, ln_b, g_attn_out, g_conv_out, w_out)


def _ffn_kernel(h_ref, g_ref, wg_ref, wu_ref, wd_ref, o_ref, f_ref):
    @pl.when(pl.program_id(1) == 0)
    def _():
        x = h_ref[...]
        f_ref[...] = _rms_scale(x, g_ref[...]).astype(BF16)
        o_ref[...] = x

    f = f_ref[...]
    gt = _dot(f, wg_ref[...])
    ut = _dot(f, wu_ref[...])
    act = (gt * _sigmoid(gt) * ut).astype(BF16)
    o_ref[...] += _dot(act, wd_ref[...])


def _ffn(h, g_ffn, w_gate, w_up, w_down):
    m = h.shape[0]
    tm = min(1024, m)
    tf = 512
    est = (4 * tm * D_MODEL * 4 + tm * D_MODEL * 2 + 2 * 3 * D_MODEL * tf * 2 + 3 * tm * tf * 4)
    return pl.pallas_call(
        _ffn_kernel,
        grid=(m // tm, D_FF // tf),
        in_specs=[pl.BlockSpec((tm, D_MODEL), lambda i, j: (i, 0)), _resident((1, D_MODEL)),
                  pl.BlockSpec((D_MODEL, tf), lambda i, j: (0, j)),
                  pl.BlockSpec((D_MODEL, tf), lambda i, j: (0, j)),
                  pl.BlockSpec((tf, D_MODEL), lambda i, j: (j, 0))],
        out_specs=pl.BlockSpec((tm, D_MODEL), lambda i, j: (i, 0)),
        out_shape=jax.ShapeDtypeStruct((m, D_MODEL), F32),
        scratch_shapes=[pltpu.VMEM((tm, D_MODEL), BF16)],
        compiler_params=pltpu.CompilerParams(dimension_semantics=("parallel", "arbitrary"),
                                             vmem_limit_bytes=_vmem_limit(est)),
        name="ffn",
    )(h, g_ffn, w_gate, w_up, w_down)


def _ple_kernel(h_ref, p_ref, g_ref, wpg_ref, wp_ref, gf_ref, o_ref, *, final):
    x = h_ref[...]
    gate = _sigmoid(_dot(_rms_scale(x, g_ref[...]).astype(BF16), wpg_ref[...]))
    pe = _dot(p_ref[...].astype(BF16), wp_ref[...])
    y = x + pe * gate
    if final:
        y = _rms_scale(y, gf_ref[...])
    o_ref[...] = y


def _ple(h, p, g_ple, w_ple_gate, w_ple, g_final, final):
    m = h.shape[0]
    tm = min(512, m)
    row = lambda i: (i, 0)
    est = 4 * tm * D_MODEL * 4 + 2 * tm * PLE_DIM * 4 + (D_MODEL + PLE_DIM) * D_MODEL * 2 + 4 * tm * D_MODEL * 4
    return pl.pallas_call(
        functools.partial(_ple_kernel, final=final),
        grid=(m // tm,),
        in_specs=[pl.BlockSpec((tm, D_MODEL), row), pl.BlockSpec((tm, PLE_DIM), row), _resident((1, D_MODEL)),
                  _resident((D_MODEL, D_MODEL)), _resident((PLE_DIM, D_MODEL)), _resident((1, D_MODEL))],
        out_specs=pl.BlockSpec((tm, D_MODEL), row),
        out_shape=jax.ShapeDtypeStruct((m, D_MODEL), F32),
        compiler_params=pltpu.CompilerParams(dimension_semantics=("parallel",),
                                             vmem_limit_bytes=_vmem_limit(est)),
        name="ple",
    )(h, p, g_ple, w_ple_gate, w_ple, g_final)


def _rope_tables(seq):
    half = ROT_DIM // 2
    inv_freq = np.exp(-math.log(ROPE_THETA) * np.arange(0, ROT_DIM, 2, dtype=np.float64) / ROT_DIM)
    ang = np.arange(seq, dtype=np.float64)[:, None] * inv_freq[None, :]
    cos = jnp.asarray(np.cos(ang), F32)
    sin = jnp.asarray(np.sin(ang), F32)
    ones = jnp.ones((seq, HEAD_DIM - ROT_DIM), F32)
    zeros = lambda n: jnp.zeros((seq, n), F32)
    c = jnp.concatenate([cos, cos, ones], axis=1)
    sa = jnp.concatenate([-sin, zeros(HEAD_DIM - half)], axis=1)
    sb = jnp.concatenate([zeros(half), sin, zeros(HEAD_DIM - ROT_DIM)], axis=1)
    return c, sa, sb


def kernel(x_prompt, x_sample, p_prompt, p_sample, g_mix, w_in, sink, conv_w, conv_b, conv_ln_g, conv_ln_b,
           g_attn_out, g_conv_out, w_out, g_ffn, w_gate, w_up, w_down, g_ple, w_ple_gate, w_ple, g_final):
    depth = w_in.shape[0]
    w_in, w_out, w_gate, w_up, w_down, w_ple_gate, w_ple = (
        w.astype(BF16) for w in (w_in, w_out, w_gate, w_up, w_down, w_ple_gate, w_ple))
    vec = lambda a, i: a[i][None, :]
    gf = g_final[None, :]

    def trunk(x, p):
        b, seq, _ = x.shape
        m = b * seq
        tables = _rope_tables(seq)
        h = x.reshape(m, D_MODEL)
        for i in range(depth):
            q, k, v, u = _in_proj(h, vec(g_mix, i), w_in[i], *tables, seq)
            r3 = lambda a: a.reshape(b, seq, a.shape[-1])
            h = _mix(r3(h), r3(q), r3(k), r3(v), r3(u), sink[i], conv_w[i], vec(conv_b, i), vec(conv_ln_g, i),
                     vec(conv_ln_b, i), vec(g_attn_out, i), vec(g_conv_out, i), w_out[i]).reshape(m, D_MODEL)
            h = _ffn(h, vec(g_ffn, i), w_gate[i], w_up[i], w_down[i])
            h = _ple(h, p[i].reshape(m, PLE_DIM), vec(g_ple, i), w_ple_gate[i], w_ple[i], gf, i == depth - 1)
        return h.reshape(b, seq, D_MODEL)

    return (trunk(x_prompt, p_prompt), trunk(x_sample, p_sample))
```
